```python
import functools
import jax
import jax.numpy as jnp
from jax import lax
import numpy as np

D_MODEL = 2048
BATCH = 16
SEQ = 2048
DEPTH = 1
DEC_BATCH = 32
DEC_SEQ = 4
PAST_LEN = 16384
PAGE_SIZE = 128

ATT_HEAD_DIM = 128
ATT_WIDTH = D_MODEL // 2
ATT_HEADS = ATT_WIDTH // ATT_HEAD_DIM
HG_DK = 128
HG_DV = 128
HG_WIDTH = D_MODEL - ATT_WIDTH
HG_HEADS = HG_WIDTH // HG_DK
MIX_WIDTH = ATT_WIDTH + HG_WIDTH
MOBA_BLOCK = 256
MOBA_TOPK = 3
Q_BLOCK = 16
HG_CHUNK = 32
D_FF = 4 * D_MODEL
N_MOD = 6
EPS = 1e-6
IN_WIDTH = 3 * ATT_WIDTH + 4 * HG_WIDTH
SPLIT_POINTS = [ATT_WIDTH, 2 * ATT_WIDTH, 3 * ATT_WIDTH, 3 * ATT_WIDTH + HG_WIDTH,
                3 * ATT_WIDTH + 2 * HG_WIDTH, 3 * ATT_WIDTH + 3 * HG_WIDTH]

kernel_name = "hymba_moba_hgrn2_adaln_decode_step"


def _rms(x, gain):
    x32 = x.astype(jnp.float32)
    y = x32 * lax.rsqrt(jnp.mean(x32 * x32, axis=-1, keepdims=True) + EPS)
    return (y * gain.astype(jnp.float32)).astype(x.dtype)


def _moba_prompt(q, k, v):
    B, H, S, hd = q.shape
    nb = -(-S // MOBA_BLOCK)
    s_pad = nb * MOBA_BLOCK
    pad = ((0, 0), (0, 0), (0, s_pad - S), (0, 0))
    q, k, v = jnp.pad(q, pad), jnp.pad(k, pad), jnp.pad(v, pad)
    kb = k.reshape(B, H, nb, MOBA_BLOCK, hd)
    vb = v.reshape(B, H, nb, MOBA_BLOCK, hd)
    n_sel = max(min(MOBA_TOPK, nb - 1), 1)
    qblk = jnp.arange(s_pad) // MOBA_BLOCK
    kmean = jnp.mean(kb.astype(jnp.float32), axis=3)
    gate = jnp.einsum('bhsd,bhnd->bhsn', q.astype(jnp.float32), kmean)
    past = jnp.arange(nb)[None, :] < qblk[:, None]
    gate = jnp.where(past, gate, -jnp.inf)
    vals, idx = lax.top_k(gate, n_sel)
    valid = jnp.isfinite(vals)
    nq = s_pad // Q_BLOCK

    def to_blocks(t):
        return jnp.moveaxis(t.reshape(B, H, nq, Q_BLOCK, *t.shape[3:]), 2, 0)

    b_ix = jnp.arange(B)[:, None, None, None]
    h_ix = jnp.arange(H)[None, :, None, None]
    scale = hd ** -0.5
    n_keys = n_sel * MOBA_BLOCK

    def one_block(args):
        start, qc, ic, mc = args
        qpos = start + jnp.arange(Q_BLOCK)
        blk = start // MOBA_BLOCK
        own_k = lax.dynamic_index_in_dim(kb, blk, axis=2, keepdims=False)
        own_v = lax.dynamic_index_in_dim(vb, blk, axis=2, keepdims=False)
        kpos = blk * MOBA_BLOCK + jnp.arange(MOBA_BLOCK)
        s_own = jnp.einsum('bhqd,bhnd->bhqn', qc, own_k).astype(jnp.float32) * scale
        s_own = jnp.where(kpos[None, :] <= qpos[:, None], s_own, -jnp.inf)
        sel_k = kb[b_ix, h_ix, ic].reshape(B, H, Q_BLOCK, n_keys, hd)
        sel_v = vb[b_ix, h_ix, ic].reshape(B, H, Q_BLOCK, n_keys, hd)
        s_sel = jnp.einsum('bhqd,bhqnd->bhqn', qc, sel_k).astype(jnp.float32) * scale
        s_sel = jnp.where(jnp.repeat(mc, MOBA_BLOCK, axis=-1), s_sel, -jnp.inf)
        p = jax.nn.softmax(jnp.concatenate([s_sel, s_own], axis=-1), axis=-1).astype(v.dtype)
        return (jnp.einsum('bhqn,bhqnd->bhqd', p[..., :n_keys], sel_v)
                + jnp.einsum('bhqn,bhnd->bhqd', p[..., n_keys:], own_v))

    starts = jnp.arange(nq, dtype=jnp.int32) * Q_BLOCK
    out = lax.map(one_block, (starts, to_blocks(q), to_blocks(idx), to_blocks(valid)))
    return jnp.moveaxis(out, 0, 2).reshape(B, H, s_pad, hd)[:, :, :S]


def _moba_sample(cache_k, cache_v, page_table, layer, q, k_new, v_new):
    DB, H, L, hd = q.shape
    ppb = MOBA_BLOCK // PAGE_SIZE
    n_pages = PAST_LEN // PAGE_SIZE
    qb = PAST_LEN // MOBA_BLOCK
    r_pages = n_pages - qb * ppb
    n_sel = min(MOBA_TOPK, qb)
    scale = hd ** -0.5
    segs = []
    s_new = jnp.einsum('bhld,bhmd->bhlm', q, k_new).astype(jnp.float32) * scale
    causal = jnp.tril(jnp.ones((L, L), dtype=bool))
    segs.append((jnp.where(causal, s_new, -jnp.inf), v_new, False))
    if n_sel > 0:
        kp = cache_k[layer, page_table[:, :qb * ppb]]
        kmean = jnp.mean(kp.astype(jnp.float32).reshape(DB, qb, ppb, H, PAGE_SIZE, hd), axis=(2, 4))
        gate = jnp.einsum('bhld,bnhd->bhln', q.astype(jnp.float32), kmean)
        _, idx = lax.top_k(gate, n_sel)
        logical = idx[..., None] * ppb + jnp.arange(ppb)
        b_ix = jnp.arange(DB)[:, None, None, None, None]
        h_ix = jnp.arange(H)[None, :, None, None, None]
        phys = page_table[b_ix, logical]
        sel_k = cache_k[layer, phys, h_ix].reshape(DB, H, L, n_sel * MOBA_BLOCK, hd)
        sel_v = cache_v[layer, phys, h_ix].reshape(DB, H, L, n_sel * MOBA_BLOCK, hd)
        s_sel = jnp.einsum('bhld,bhlnd->bhln', q, sel_k).astype(jnp.float32) * scale
        segs.append((s_sel, sel_v, True))
    if r_pages > 0:
        own_pages = page_table[:, qb * ppb:]
        ok = cache_k[layer, own_pages].transpose(0, 2, 1, 3, 4).reshape(DB, H, r_pages * PAGE_SIZE, hd)
        ov = cache_v[layer, own_pages].transpose(0, 2, 1, 3, 4).reshape(DB, H, r_pages * PAGE_SIZE, hd)
        s_own = jnp.einsum('bhld,bhnd->bhln', q, ok).astype(jnp.float32) * scale
        segs.append((s_own, ov, False))
    p = jax.nn.softmax(jnp.concatenate([s for s, _, _ in segs], axis=-1), axis=-1).astype(v_new.dtype)
    out = None
    off = 0
    for s, val, per_query in segs:
        n = s.shape[-1]
        pp = p[..., off:off + n]
        off += n
        term = (jnp.einsum('bhln,bhlnd->bhld', pp, val) if per_query
                else jnp.einsum('bhln,bhnd->bhld', pp, val))
        out = term if out is None else out + term
    return out


def _hgrn2(q, f_logit, i, lb, s0):
    B, L = q.shape[:2]
    lb = lb.astype(jnp.float32)
    forget = lb + (1.0 - lb) * jax.nn.sigmoid(f_logit.astype(jnp.float32))
    log_f = jnp.log(forget)
    k = 1.0 - forget
    qs = jax.nn.silu(q.astype(jnp.float32)) * (HG_DK ** -0.5)
    v = i.astype(jnp.float32)
    C = HG_CHUNK if L % HG_CHUNK == 0 else L
    n = L // C

    def to_chunks(t):
        return t.reshape(B, n, C, HG_HEADS, t.shape[-1]).transpose(1, 0, 3, 2, 4)

    causal = jnp.tril(jnp.ones((C, C), dtype=bool))

    def step(S, xs):
        qc, kc, vc, gc = xs
        b = jnp.cumsum(gc, axis=2)
        diff = b[:, :, :, None, :] - b[:, :, None, :, :]
        decay = jnp.exp(jnp.where(causal[:, :, None], diff, -jnp.inf))
        A = jnp.einsum('bhtsd,bhsd->bhts', decay * qc[:, :, :, None, :], kc)
        o = (jnp.einsum('bhts,bhsv->bhtv', A, vc)
             + jnp.einsum('bhtd,bhdv->bhtv', qc * jnp.exp(b), S))
        b_end = b[:, :, -1:, :]
        S = (jnp.exp(b_end[:, :, 0, :])[..., None] * S
             + jnp.einsum('bhsd,bhsv->bhdv', kc * jnp.exp(b_end - b), vc))
        return S, o

    S, o = lax.scan(step, s0.astype(jnp.float32),
                    (to_chunks(qs), to_chunks(k), to_chunks(v), to_chunks(log_f)))
    o = o.transpose(1, 0, 3, 2, 4).reshape(B, L, HG_HEADS, HG_DV)
    return o, S.astype(s0.dtype)


def _layer(x, c, lp, lb, attend, s0):
    B, L, _ = x.shape
    mod = jax.nn.silu(c) @ lp['w_ada'] + lp['b_ada']
    sh1, sc1, g1, sh2, sc2, g2 = jnp.split(mod[:, None, :], N_MOD, axis=-1)
    h = _rms(x, lp['norm1']) * (1 + sc1) + sh1
    proj = h @ lp['w_in']
    qa, ka, va, qh, fh, ih, gh = jnp.split(proj, SPLIT_POINTS, axis=-1)

    def heads(t, nh):
        return t.reshape(B, L, nh, -1)

    qa = _rms(heads(qa, ATT_HEADS), lp['q_norm']).transpose(0, 2, 1, 3)
    ka = _rms(heads(ka, ATT_HEADS), lp['k_norm']).transpose(0, 2, 1, 3)
    va = heads(va, ATT_HEADS).transpose(0, 2, 1, 3)
    ao = attend(qa, ka, va)
    ao = _rms(ao.transpose(0, 2, 1, 3), lp['attn_out']).reshape(B, L, ATT_WIDTH)
    ho, s_new = _hgrn2(heads(qh, HG_HEADS), heads(fh, HG_HEADS), heads(ih, HG_HEADS), lb, s0)
    ho = (_rms(ho, lp['hgrn_out']) * jax.nn.silu(heads(gh, HG_HEADS))).astype(x.dtype)
    ho = ho.reshape(B, L, HG_WIDTH)
    x = x + g1 * (jnp.concatenate([ao, ho], axis=-1) @ lp['w_out'])
    h2 = _rms(x, lp['norm2']) * (1 + sc2) + sh2
    x = x + g2 * (jnp.square(jax.nn.relu(h2 @ lp['w_ff1'])) @ lp['w_ff2'])
    return x, ka, va, s_new


def setup_inputs(seed: int = 0) -> dict:
    key = jax.random.key(seed)
    ks = jax.random.split(key, 24)
    f32 = jnp.float32
    n_pages = PAST_LEN // PAGE_SIZE
    n_used = DEC_BATCH * n_pages
    n_phys = n_used + n_used // 4

    def nrm(k, shape, s):
        return jax.random.normal(k, shape, f32) * s

    page_table = jax.random.permutation(ks[7], n_phys)[:n_used].reshape(DEC_BATCH, n_pages).astype(jnp.int32)
    return {
        'x_prompt': nrm(ks[0], (BATCH, SEQ, D_MODEL), 1.0),
        'x_sample': nrm(ks[1], (DEC_BATCH, DEC_SEQ, D_MODEL), 1.0),
        'c_prompt': nrm(ks[2], (BATCH, D_MODEL), 1.0),
        'c_sample': nrm(ks[3], (DEC_BATCH, D_MODEL), 1.0),
        'cache_k': nrm(ks[4], (DEPTH, n_phys, ATT_HEADS, PAGE_SIZE, ATT_HEAD_DIM), 1.0),
        'cache_v': nrm(ks[5], (DEPTH, n_phys, ATT_HEADS, PAGE_SIZE, ATT_HEAD_DIM), 1.0),
        'state_hgrn': nrm(ks[6], (DEPTH, DEC_BATCH, HG_HEADS, HG_DK, HG_DV), 0.5),
        'page_table': page_table,
        'w_ada': nrm(ks[8], (DEPTH, D_MODEL, N_MOD * D_MODEL), 0.5 * D_MODEL ** -0.5),
        'b_ada': nrm(ks[9], (DEPTH, N_MOD * D_MODEL), 0.02),
        'norm1_gain': 1.0 + nrm(ks[10], (DEPTH, D_MODEL), 0.02),
        'norm2_gain': 1.0 + nrm(ks[11], (DEPTH, D_MODEL), 0.02),
        'w_in': nrm(ks[12], (DEPTH, D_MODEL, IN_WIDTH), D_MODEL ** -0.5),
        'q_norm_gain': 1.0 + nrm(ks[13], (DEPTH, ATT_HEAD_DIM), 0.02),
        'k_norm_gain': 1.0 + nrm(ks[14], (DEPTH, ATT_HEAD_DIM), 0.02),
        'attn_out_gain': 1.0 + nrm(ks[15], (DEPTH, ATT_HEADS, ATT_HEAD_DIM), 0.02),
        'hgrn_lb_logits': nrm(ks[16], (DEPTH + 1, HG_HEADS, HG_DK), 0.1),
        'hgrn_out_gain': 1.0 + nrm(ks[17], (DEPTH, HG_HEADS, HG_DV), 0.02),
        'w_out': nrm(ks[18], (DEPTH, MIX_WIDTH, D_MODEL), MIX_WIDTH ** -0.5),
        'w_ff1': nrm(ks[19], (DEPTH, D_MODEL, D_FF), D_MODEL ** -0.5),
        'w_ff2': nrm(ks[20], (DEPTH, D_FF, D_MODEL), D_FF ** -0.5),
    }


def reference(x_prompt, x_sample, c_prompt, c_sample, cache_k, cache_v, state_hgrn, page_table,
              w_ada, b_ada, norm1_gain, norm2_gain, w_in, q_norm_gain, k_norm_gain, attn_out_gain,
              hgrn_lb_logits, hgrn_out_gain, w_out, w_ff1, w_ff2):
    lb_all = jnp.cumsum(jax.nn.softmax(hgrn_lb_logits.astype(jnp.float32), axis=0), axis=0)
    xp, xs = x_prompt, x_sample
    kp_l, vp_l, sp_l, ks_l, vs_l, ss_l = [], [], [], [], [], []
    for l in range(DEPTH):
        lp = {'w_ada': w_ada[l], 'b_ada': b_ada[l], 'norm1': norm1_gain[l], 'norm2': norm2_gain[l],
              'w_in': w_in[l], 'q_norm': q_norm_gain[l], 'k_norm': k_norm_gain[l],
              'attn_out': attn_out_gain[l], 'hgrn_out': hgrn_out_gain[l], 'w_out': w_out[l],
              'w_ff1': w_ff1[l], 'w_ff2': w_ff2[l]}
        s0 = jnp.zeros((xp.shape[0], HG_HEADS, HG_DK, HG_DV), jnp.float32)
        xp, kp, vp, sp = _layer(xp, c_prompt, lp, lb_all[l], _moba_prompt, s0)
        attend_sample = functools.partial(_moba_sample, cache_k, cache_v, page_table, l)
        xs, kn, vn, sn = _layer(xs, c_sample, lp, lb_all[l], attend_sample, state_hgrn[l])
        kp_l.append(kp)
        vp_l.append(vp)
        sp_l.append(sp)
        ks_l.append(kn)
        vs_l.append(vn)
        ss_l.append(sn)
    return (xp, xs, jnp.stack(kp_l), jnp.stack(vp_l), jnp.stack(sp_l),
            jnp.stack(ks_l), jnp.stack(vs_l), jnp.stack(ss_l))
```

```python
import functools

import jax
import jax.numpy as jnp
from jax import lax
from jax.experimental import pallas as pl
from jax.experimental.pallas import tpu as pltpu

EPS = 1e-6
HEAD_DIM = 128
MOBA_BLOCK = 256
MOBA_TOPK = 3
N_MOD = 6
N_SEG = 7
HGRN_CHUNK = 64
HGRN_SAFE_LOG_DECAY = 60.0
VMEM_LIMIT_BYTES = 56 * 1024 * 1024

F32 = jnp.float32
BF16 = jnp.bfloat16
NT_DIMS = (((1,), (1,)), ((), ()))
TN_DIMS = (((0,), (0,)), ((), ()))


def _params(*semantics):
    return pltpu.CompilerParams(dimension_semantics=semantics, vmem_limit_bytes=VMEM_LIMIT_BYTES)


def _silu(x):
    return x * jax.nn.sigmoid(x)


def _rms_rows(x, gain):
    return x * lax.rsqrt(jnp.mean(x * x, axis=-1, keepdims=True) + EPS) * gain


def _ada_kernel(c_ref, w_ref, b_ref, o_ref):
    a = _silu(c_ref[...]).astype(BF16)
    o_ref[...] = jnp.dot(a, w_ref[...].astype(BF16), preferred_element_type=F32) + b_ref[...]


def _ada_modulation(c, w_ada, b_ada):
    R, D = c.shape
    N = w_ada.shape[1]
    tn = min(512, N)
    return pl.pallas_call(
        _ada_kernel,
        grid=(N // tn,),
        in_specs=[pl.BlockSpec((R, D), lambda j: (0, 0)),
                  pl.BlockSpec((D, tn), lambda j: (0, j)),
                  pl.BlockSpec((1, tn), lambda j: (0, j))],
        out_specs=pl.BlockSpec((R, tn), lambda j: (0, j)),
        out_shape=jax.ShapeDtypeStruct((R, N), F32),
        compiler_params=_params("arbitrary"),
        name="ada_modulation",
    )(c, w_ada, b_ada.reshape(1, N))


def _in_proj_kernel(x_ref, sc_ref, sh_ref, g_ref, w_ref, qg_ref, kg_ref,
                    q_out, k_out, v_out, hg_out, h_scr, *, head_major, n_heads):
    j = pl.program_id(1)

    @pl.when(j == 0)
    def _():
        y = _rms_rows(x_ref[...], g_ref[...])
        h_scr[...] = (y * (1.0 + sc_ref[0]) + sh_ref[0]).astype(BF16)

    res = jnp.dot(h_scr[...], w_ref[...], preferred_element_type=F32)

    def emit(out_ref, lead, fn):
        for hh in range(n_heads):
            t = fn(res[:, hh * HEAD_DIM:(hh + 1) * HEAD_DIM])
            if head_major:
                out_ref[lead + (hh,)] = t
            else:
                out_ref[lead + (slice(None), slice(hh * HEAD_DIM, (hh + 1) * HEAD_DIM))] = t

    @pl.when(j == 0)
    def _():
        emit(q_out, (0,) if head_major else (), lambda t: _rms_rows(t, qg_ref[...]))

    @pl.when(j == 1)
    def _():
        emit(k_out, (0,) if head_major else (), lambda t: _rms_rows(t, kg_ref[...]))

    @pl.when(j == 2)
    def _():
        emit(v_out, (0,) if head_major else (), lambda t: t)

    @pl.when(j >= 3)
    def _():
        emit(hg_out, (0, 0) if head_major else (0,), lambda t: t)


def _in_proj(x2d, sc, sh, gain, w_bf16, q_gain, k_gain, *, rows_per_batch, head_major):
    M, D = x2d.shape
    W = w_bf16.shape[1] // N_SEG
    n_heads = W // HEAD_DIM
    tm = min(512, rows_per_batch) if head_major else M
    n_bt = rows_per_batch // tm if head_major else 1
    nb = M // rows_per_batch if head_major else 1

    mod_spec = pl.BlockSpec((1, sc.shape[1], D), lambda i, j: (i // n_bt, 0, 0))
    if head_major:
        o4 = pl.BlockSpec((1, n_heads, tm, HEAD_DIM), lambda i, j: (i // n_bt, 0, i % n_bt, 0))
        o5 = pl.BlockSpec((1, 1, n_heads, tm, HEAD_DIM),
                          lambda i, j: (jnp.maximum(j - 3, 0), i // n_bt, 0, i % n_bt, 0))
        s4 = jax.ShapeDtypeStruct((nb, n_heads, rows_per_batch, HEAD_DIM), F32)
        s5 = jax.ShapeDtypeStruct((4, nb, n_heads, rows_per_batch, HEAD_DIM), F32)
    else:
        o4 = pl.BlockSpec((tm, W), lambda i, j: (i, 0))
        o5 = pl.BlockSpec((1, tm, W), lambda i, j: (jnp.maximum(j - 3, 0), i, 0))
        s4 = jax.ShapeDtypeStruct((M, W), F32)
        s5 = jax.ShapeDtypeStruct((4, M, W), F32)
    return pl.pallas_call(
        functools.partial(_in_proj_kernel, head_major=head_major, n_heads=n_heads),
        grid=(M // tm, N_SEG),
        in_specs=[pl.BlockSpec((tm, D), lambda i, j: (i, 0)),
                  mod_spec, mod_spec,
                  pl.BlockSpec((1, D), lambda i, j: (0, 0)),
                  pl.BlockSpec((D, W), lambda i, j: (0, j)),
                  pl.BlockSpec((1, HEAD_DIM), lambda i, j: (0, 0)),
                  pl.BlockSpec((1, HEAD_DIM), lambda i, j: (0, 0))],
        out_specs=[o4, o4, o4, o5],
        out_shape=[s4, s4, s4, s5],
        scratch_shapes=[pltpu.VMEM((tm, D), BF16)],
        compiler_params=_params("arbitrary", "arbitrary"),
        name="in_proj",
    )(x2d, sc, sh, gain.reshape(1, D), w_bf16, q_gain.reshape(1, HEAD_DIM), k_gain.reshape(1, HEAD_DIM))


def _moba_prompt_kernel(q_ref, k_ref, v_ref, g_ref, o_ref, kb_scr, vb_scr, km_scr,
                        m_scr, l_scr, acc_scr, *, nb):
    qi = pl.program_id(2)
    BLK = MOBA_BLOCK

    @pl.when(qi == 0)
    def _():
        k = k_ref[0, 0]
        kb_scr[...] = k.astype(BF16)
        vb_scr[...] = v_ref[0, 0].astype(BF16)
        km_scr[...] = jnp.zeros_like(km_scr)
        for n in range(nb):
            km_scr[n:n + 1, :] = jnp.mean(k[n * BLK:(n + 1) * BLK], axis=0, keepdims=True)

    q = q_ref[0, 0]
    gate = lax.dot_general(q, km_scr[...], NT_DIMS, precision=lax.Precision.HIGHEST,
                           preferred_element_type=F32)
    lane = lax.broadcasted_iota(jnp.int32, gate.shape, 1)
    past = lane < qi
    gate = jnp.where(past, gate, -jnp.inf)
    rank = jnp.zeros(gate.shape, jnp.int32)
    for m in range(nb - 1):
        gm = gate[:, m:m + 1]
        beats = (gm > gate) | ((gm == gate) & (m < lane))
        rank = rank + jnp.where(beats, 1, 0)
    n_sel = max(min(MOBA_TOPK, nb - 1), 1)
    bias = jnp.where(past & (rank < n_sel), 0.0, -jnp.inf)

    qb = (q * (HEAD_DIM ** -0.5)).astype(BF16)

    def scores(n):
        return lax.dot_general(qb, kb_scr[pl.ds(n * BLK, BLK), :], NT_DIMS, preferred_element_type=F32)

    row = lax.broadcasted_iota(jnp.int32, (BLK, BLK), 0)
    col = lax.broadcasted_iota(jnp.int32, (BLK, BLK), 1)
    own = pl.multiple_of(qi * BLK, BLK)
    s = lax.dot_general(qb, kb_scr[pl.ds(own, BLK), :], NT_DIMS, preferred_element_type=F32)
    s = jnp.where(col <= row, s, -jnp.inf)
    m0 = jnp.max(s, axis=-1, keepdims=True)
    p = jnp.exp(s - m0)
    m_scr[...] = m0
    l_scr[...] = jnp.sum(p, axis=-1, keepdims=True)
    acc_scr[...] = jnp.dot(p.astype(BF16), vb_scr[pl.ds(own, BLK), :], preferred_element_type=F32)

    for n in range(nb - 1):
        @pl.when(n < qi)
        def _(n=n):
            s = scores(n) + bias[:, n:n + 1]
            m_old = m_scr[...]
            m_new = jnp.maximum(m_old, jnp.max(s, axis=-1, keepdims=True))
            alpha = jnp.exp(m_old - m_new)
            p = jnp.exp(s - m_new)
            m_scr[...] = m_new
            l_scr[...] = alpha * l_scr[...] + jnp.sum(p, axis=-1, keepdims=True)
            acc_scr[...] = alpha * acc_scr[...] + jnp.dot(
                p.astype(BF16), vb_scr[n * BLK:(n + 1) * BLK, :], preferred_element_type=F32)

    o = acc_scr[...] / l_scr[...]
    o_ref[0] = _rms_rows(o, g_ref[0]).astype(o_ref.dtype)


def _moba_prompt(q, k, v, out_gain):
    B, H, S, hd = q.shape
    assert S % MOBA_BLOCK == 0 and hd == HEAD_DIM
    nb = S // MOBA_BLOCK
    assert 2 <= nb <= HEAD_DIM
    kv_spec = pl.BlockSpec((1, 1, S, hd), lambda b, h, i: (b, h, 0, 0))
    return pl.pallas_call(
        functools.partial(_moba_prompt_kernel, nb=nb),
        grid=(B, H, nb),
        in_specs=[pl.BlockSpec((1, 1, MOBA_BLOCK, hd), lambda b, h, i: (b, h, i, 0)),
                  kv_spec, kv_spec,
                  pl.BlockSpec((1, 1, hd), lambda b, h, i: (h, 0, 0))],
        out_specs=pl.BlockSpec((1, MOBA_BLOCK, hd), lambda b, h, i: (b, i, h)),
        out_shape=jax.ShapeDtypeStruct((B, S, H * hd), BF16),
        scratch_shapes=[pltpu.VMEM((S, hd), BF16), pltpu.VMEM((S, hd), BF16),
                        pltpu.VMEM((HEAD_DIM, hd), F32),
                        pltpu.VMEM((MOBA_BLOCK, 1), F32), pltpu.VMEM((MOBA_BLOCK, 1), F32),
                        pltpu.VMEM((MOBA_BLOCK, hd), F32)],
        compiler_params=_params("arbitrary", "arbitrary", "arbitrary"),
        name="moba_prompt",
    )(q, k, v, out_gain.reshape(H, 1, hd))


def _hgrn_kernel(*refs, layer, has_s0, C):
    if has_s0:
        lbl_ref, q_ref, f_ref, i_ref, g_ref, gain_ref, s0_ref, o_ref, s_out_ref, st_scr, b_scr, qs_scr, oin_scr = refs
    else:
        lbl_ref, q_ref, f_ref, i_ref, g_ref, gain_ref, o_ref, s_out_ref, st_scr, b_scr, qs_scr, oin_scr = refs
    c = pl.program_id(2)

    @pl.when(c == 0)
    def _():
        if has_s0:
            st_scr[...] = s0_ref[0, 0].T
        else:
            st_scr[...] = jnp.zeros_like(st_scr)

    logits = lbl_ref[:, 0, 0, :]
    e = jnp.exp(logits - jnp.max(logits, axis=0, keepdims=True))
    lb = jnp.sum(e[:layer + 1], axis=0, keepdims=True) / jnp.sum(e, axis=0, keepdims=True)

    forget = lb + (1.0 - lb) * jax.nn.sigmoid(f_ref[0, 0, 0])
    log_f = jnp.log(forget)
    kk = 1.0 - forget
    qs = _silu(q_ref[0, 0, 0]) * (HEAD_DIM ** -0.5)
    v = i_ref[0, 0, 0]
    vb = v.astype(BF16)

    row = lax.broadcasted_iota(jnp.int32, (C, C), 0)
    col = lax.broadcasted_iota(jnp.int32, (C, C), 1)
    tri = jnp.where(col <= row, 1.0, 0.0).astype(F32)
    b = jnp.dot(tri, log_f, precision=lax.Precision.HIGHEST, preferred_element_type=F32)
    b_end = b[C - 1:C, :]
    max_decay = jnp.max(-b_end)

    st = st_scr[...]
    q_in = qs * jnp.exp(b)
    inter = lax.dot_general(q_in.astype(BF16), st.astype(BF16), NT_DIMS, preferred_element_type=F32)
    k_st = kk * jnp.exp(b_end - b)
    st_new = jnp.exp(b_end) * st + lax.dot_general(vb, k_st.astype(BF16), TN_DIMS, preferred_element_type=F32)
    st_scr[...] = st_new

    @pl.when(max_decay <= HGRN_SAFE_LOG_DECAY)
    def _():
        k_t = kk * jnp.exp(-b)
        a = lax.dot_general(q_in.astype(BF16), k_t.astype(BF16), NT_DIMS, preferred_element_type=F32)
        a = jnp.where(col <= row, a, 0.0)
        oin_scr[...] = jnp.dot(a.astype(BF16), vb, preferred_element_type=F32)

    @pl.when(max_decay > HGRN_SAFE_LOG_DECAY)
    def _():
        b_scr[...] = b
        qs_scr[...] = qs
        srow = lax.broadcasted_iota(jnp.int32, (C, HEAD_DIM), 0)

        def body(t, carry):
            bt = b_scr[pl.ds(t, 1), :]
            qt = qs_scr[pl.ds(t, 1), :]
            decay = jnp.exp(jnp.where(srow <= t, bt - b, -jnp.inf))
            a_col = jnp.sum(decay * kk * qt, axis=-1, keepdims=True)
            oin_scr[pl.ds(t, 1), :] = jnp.sum(a_col * v, axis=0, keepdims=True)
            return carry

        lax.fori_loop(0, C, body, 0)

    o = oin_scr[...] + inter
    o_ref[0] = (_rms_rows(o, gain_ref[0]) * _silu(g_ref[0, 0, 0])).astype(o_ref.dtype)

    @pl.when(c == pl.num_programs(2) - 1)
    def _():
        s_out_ref[0, 0] = st_new.T


def _hgrn(hg, lb_logits, out_gain, s0, *, layer):
    _, B, Hh, L, d = hg.shape
    assert d == HEAD_DIM
    C = HGRN_CHUNK if L % HGRN_CHUNK == 0 else L
    nl = lb_logits.shape[0]
    has_s0 = s0 is not None

    def seg(k):
        return pl.BlockSpec((1, 1, 1, C, d), lambda b, h, c, k=k: (k, b, h, c, 0))

    in_specs = [pl.BlockSpec((nl, 1, 1, d), lambda b, h, c: (0, h, 0, 0)),
                seg(0), seg(1), seg(2), seg(3),
                pl.BlockSpec((1, 1, d), lambda b, h, c: (h, 0, 0))]
    args = [lb_logits.reshape(nl, Hh, 1, d), hg, hg, hg, hg, out_gain.reshape(Hh, 1, d)]
    if has_s0:
        in_specs.append(pl.BlockSpec((1, 1, d, d), lambda b, h, c: (b, h, 0, 0)))
        args.append(s0)
    return pl.pallas_call(
        functools.partial(_hgrn_kernel, layer=layer, has_s0=has_s0, C=C),
        grid=(B, Hh, L // C),
        in_specs=in_specs,
        out_specs=[pl.BlockSpec((1, C, d), lambda b, h, c: (b, c, h)),
                   pl.BlockSpec((1, 1, d, d), lambda b, h, c: (b, h, 0, 0))],
        out_shape=[jax.ShapeDtypeStruct((B, L, Hh * d), BF16 if C % 16 == 0 else F32),
                   jax.ShapeDtypeStruct((B, Hh, d, d), F32)],
        scratch_shapes=[pltpu.VMEM((d, d), F32), pltpu.VMEM((C, d), F32),
                        pltpu.VMEM((C, d), F32), pltpu.VMEM((C, d), F32)],
        compiler_params=_params("arbitrary", "arbitrary", "arbitrary"),
        name="hgrn2",
    )(*args)


def _out_proj_kernel(a_ref, h_ref, wa_ref, wh_ref, x_ref, g1_ref, o_ref):
    y = jnp.dot(a_ref[...], wa_ref[...], preferred_element_type=F32)
    y = y + jnp.dot(h_ref[...], wh_ref[...], preferred_element_type=F32)
    o_ref[...] = x_ref[...] + g1_ref[0] * y


def _out_proj(ao, ho, w_out_bf16, x2d, g1, *, rows_per_batch):
    M, D = x2d.shape
    W = ao.shape[1]
    tm = min(512, rows_per_batch) if g1.shape[1] == 1 else M
    n_bt = rows_per_batch // tm if g1.shape[1] == 1 else 1
    return pl.pallas_call(
        _out_proj_kernel,
        grid=(M // tm,),
        in_specs=[pl.BlockSpec((tm, W), lambda i: (i, 0)),
                  pl.BlockSpec((tm, W), lambda i: (i, 0)),
                  pl.BlockSpec((W, D), lambda i: (0, 0)),
                  pl.BlockSpec((W, D), lambda i: (1, 0)),
                  pl.BlockSpec((tm, D), lambda i: (i, 0)),
                  pl.BlockSpec((1, g1.shape[1], D), lambda i: (i // n_bt, 0, 0))],
        out_specs=pl.BlockSpec((tm, D), lambda i: (i, 0)),
        out_shape=jax.ShapeDtypeStruct((M, D), F32),
        compiler_params=_params("arbitrary"),
        name="out_proj",
    )(ao, ho, w_out_bf16, w_out_bf16, x2d, g1)


def _ffn_kernel(x_ref, sc_ref, sh_ref, gain_ref, g2_ref, w1_ref, w2_ref, o_ref, h_scr):
    f = pl.program_id(1)

    @pl.when(f == 0)
    def _():
        y = _rms_rows(x_ref[...], gain_ref[...])
        h_scr[...] = (y * (1.0 + sc_ref[0]) + sh_ref[0]).astype(BF16)

    u = jnp.maximum(jnp.dot(h_scr[...], w1_ref[...], preferred_element_type=F32), 0.0)
    part = jnp.dot((u * u).astype(BF16), w2_ref[...], preferred_element_type=F32)

    @pl.when(f == 0)
    def _():
        o_ref[...] = part

    @pl.when(f > 0)
    def _():
        o_ref[...] += part

    @pl.when(f == pl.num_programs(1) - 1)
    def _():
        o_ref[...] = x_ref[...] + g2_ref[0] * o_ref[...]


def _ffn(x2d, sc, sh, gain, g2, w1_bf16, w2_bf16, *, rows_per_batch):
    M, D = x2d.shape
    F = w1_bf16.shape[1]
    per_batch = sc.shape[1] == 1
    tm = min(512, rows_per_batch) if per_batch else M
    n_bt = rows_per_batch // tm if per_batch else 1
    tf = min(512, F)
    mod_spec = pl.BlockSpec((1, sc.shape[1], D), lambda i, f: (i // n_bt, 0, 0))
    return pl.pallas_call(
        _ffn_kernel,
        grid=(M // tm, F // tf),
        in_specs=[pl.BlockSpec((tm, D), lambda i, f: (i, 0)),
                  mod_spec, mod_spec,
                  pl.BlockSpec((1, D), lambda i, f: (0, 0)),
                  mod_spec,
                  pl.BlockSpec((D, tf), lambda i, f: (0, f)),
                  pl.BlockSpec((tf, D), lambda i, f: (f, 0))],
        out_specs=pl.BlockSpec((tm, D), lambda i, f: (i, 0)),
        out_shape=jax.ShapeDtypeStruct((M, D), F32),
        scratch_shapes=[pltpu.VMEM((tm, D), BF16)],
        compiler_params=_params("arbitrary", "arbitrary"),
        name="ffn",
    )(x2d, sc, sh, gain.reshape(1, D), g2, w1_bf16, w2_bf16)


PAGES_PER_STEP = 8


def _kmean_kernel(pt_ref, *refs, ppb):
    page_refs, o_ref = refs[:-1], refs[-1]
    P = page_refs[0].shape[3]
    for g in range(len(page_refs) // ppb):
        tot = page_refs[g * ppb][0, 0].sum(axis=1)
        for p in range(1, ppb):
            tot = tot + page_refs[g * ppb + p][0, 0].sum(axis=1)
        o_ref[0, g] = tot * (1.0 / (ppb * P))


def _cache_block_means(cache_k, page_table, layer, n_blocks, ppb):
    _, _, H, P, hd = cache_k.shape
    DB = page_table.shape[0]
    pps = PAGES_PER_STEP
    while (n_blocks * ppb) % pps:
        pps //= 2
    pps = max(pps, ppb)
    bps = pps // ppb

    def page_spec(j):
        return pl.BlockSpec((1, 1, H, P, hd), lambda b, g, pt, j=j: (layer, pt[b, g * pps + j], 0, 0, 0))

    return pl.pallas_call(
        functools.partial(_kmean_kernel, ppb=ppb),
        grid_spec=pltpu.PrefetchScalarGridSpec(
            num_scalar_prefetch=1,
            grid=(DB, n_blocks // bps),
            in_specs=[page_spec(j) for j in range(pps)],
            out_specs=pl.BlockSpec((1, bps, H, hd), lambda b, g, pt: (b, g, 0, 0))),
        out_shape=jax.ShapeDtypeStruct((DB, n_blocks, H, hd), F32),
        compiler_params=_params("arbitrary", "arbitrary"),
        name="cache_block_means",
    )(page_table, *([cache_k] * pps))


def _gate_topk_kernel(q_ref, km_ref, idx_ref, *, H, n_sel, n_blocks):
    for h in range(H):
        q = q_ref[0, h]
        km = km_ref[0, :, h, :]
        gate = lax.dot_general(q, km, NT_DIMS, precision=lax.Precision.HIGHEST,
                               preferred_element_type=F32)
        lane = lax.broadcasted_iota(jnp.int32, gate.shape, 1).astype(F32)
        out_lane = lax.broadcasted_iota(jnp.int32, (gate.shape[0], HEAD_DIM), 1)
        out = jnp.zeros((gate.shape[0], HEAD_DIM), F32)
        for r in range(n_sel):
            mx = jnp.max(gate, axis=-1, keepdims=True)
            pick = jnp.min(jnp.where(gate == mx, lane, float(n_blocks)), axis=-1, keepdims=True)
            out = jnp.where(out_lane == r, pick, out)
            gate = jnp.where(lane == pick, -jnp.inf, gate)
        idx_ref[0, h] = out.astype(jnp.int32)


def _gate_topk(q, kmean, n_sel):
    DB, H, L, hd = q.shape
    n_blocks = kmean.shape[1]
    idx = pl.pallas_call(
        functools.partial(_gate_topk_kernel, H=H, n_sel=n_sel, n_blocks=n_blocks),
        grid=(DB,),
        in_specs=[pl.BlockSpec((1, H, L, hd), lambda b: (b, 0, 0, 0)),
                  pl.BlockSpec((1, n_blocks, H, hd), lambda b: (b, 0, 0, 0))],
        out_specs=pl.BlockSpec((1, H, L, HEAD_DIM), lambda b: (b, 0, 0, 0)),
        out_shape=jax.ShapeDtypeStruct((DB, H, L, HEAD_DIM), jnp.int32),
        compiler_params=_params("arbitrary"),
        name="gate_topk",
    )(q, kmean)
    return idx[..., :n_sel]


def _moba_sample_kernel(idx_ref, pt_ref, q_ref, kn_ref, vn_ref, g_ref, *refs, L, n_pg):
    k_refs, v_refs, o_ref = refs[:L * n_pg], refs[L * n_pg:2 * L * n_pg], refs[-1]
    scale = HEAD_DIM ** -0.5
    kn = kn_ref[0, 0]
    vn = vn_ref[0, 0]
    new_row = lax.broadcasted_iota(jnp.int32, (L, 1), 0)
    for l in range(L):
        q = q_ref[0, 0, l:l + 1, :] * scale
        s_new = jnp.sum(kn * q, axis=-1, keepdims=True)
        s_new = jnp.where(new_row <= l, s_new, -jnp.inf)
        s_pg = [jnp.sum(k_refs[l * n_pg + j][0, 0, 0] * q, axis=-1, keepdims=True) for j in range(n_pg)]
        m = jnp.max(s_new, axis=0, keepdims=True)
        for s in s_pg:
            m = jnp.maximum(m, jnp.max(s, axis=0, keepdims=True))
        p_new = jnp.exp(s_new - m)
        den = jnp.sum(p_new, axis=0, keepdims=True)
        acc = jnp.sum(p_new * vn, axis=0, keepdims=True)
        for j, s in enumerate(s_pg):
            p = jnp.exp(s - m)
            den = den + jnp.sum(p, axis=0, keepdims=True)
            acc = acc + jnp.sum(p * v_refs[l * n_pg + j][0, 0, 0], axis=0, keepdims=True)
        o = acc / den
        o_ref[0, l:l + 1, :] = _rms_rows(o, g_ref[0]).astype(o_ref.dtype)


def _moba_sample(q, k_new, v_new, idx, page_table, cache_k, cache_v, out_gain, layer, ppb):
    DB, H, L, hd = q.shape
    P = cache_k.shape[3]
    n_sel = idx.shape[-1]
    n_pg = n_sel * ppb

    def page_spec(l, j):
        def index_map(b, h, idx_ref, pt_ref):
            logical = idx_ref[b, h, l, j // ppb] * ppb + (j % ppb)
            return (layer, pt_ref[b, logical], h, 0, 0)
        return pl.BlockSpec((1, 1, 1, P, hd), index_map)

    tok_spec = pl.BlockSpec((1, 1, L, hd), lambda b, h, i, p: (b, h, 0, 0))
    page_specs = [page_spec(l, j) for l in range(L) for j in range(n_pg)]
    return pl.pallas_call(
        functools.partial(_moba_sample_kernel, L=L, n_pg=n_pg),
        grid_spec=pltpu.PrefetchScalarGridSpec(
            num_scalar_prefetch=2,
            grid=(DB, H),
            in_specs=[tok_spec, tok_spec, tok_spec,
                      pl.BlockSpec((1, 1, hd), lambda b, h, i, p: (h, 0, 0))] + page_specs + page_specs,
            out_specs=pl.BlockSpec((1, L, hd), lambda b, h, i, p: (b, 0, h))),
        out_shape=jax.ShapeDtypeStruct((DB, L, H * hd), F32),
        compiler_params=_params("arbitrary", "arbitrary"),
        name="moba_sample",
    )(idx, page_table, q, k_new, v_new, out_gain.reshape(H, 1, hd),
      *([cache_k] * (L * n_pg)), *([cache_v] * (L * n_pg)))


def kernel(x_prompt, x_sample, c_prompt, c_sample, cache_k, cache_v, state_hgrn, page_table, w_ada, b_ada,
           norm1_gain, norm2_gain, w_in, q_norm_gain, k_norm_gain, attn_out_gain, hgrn_lb_logits,
           hgrn_out_gain, w_out, w_ff1, w_ff2):
    B, S, D = x_prompt.shape
    DB, L, _ = x_sample.shape
    depth = w_in.shape[0]
    W = D // 2
    H = W // HEAD_DIM
    assert w_in.shape[2] == N_SEG * W and w_out.shape[1] == 2 * W
    page = cache_k.shape[3]
    n_pages = page_table.shape[1]
    past_len = n_pages * page
    ppb = MOBA_BLOCK // page
    n_past_blocks = past_len // MOBA_BLOCK
    assert n_pages == n_past_blocks * ppb and n_past_blocks >= MOBA_TOPK
    n_sel = min(MOBA_TOPK, n_past_blocks)

    xp = x_prompt.reshape(B * S, D)
    xs = x_sample.reshape(DB * L, D)
    c_all = jnp.concatenate([c_prompt, c_sample], axis=0)
    outs = [[] for _ in range(6)]
    for l in range(depth):
        w_in_b, w_out_b = w_in[l].astype(BF16), w_out[l].astype(BF16)
        w1_b, w2_b = w_ff1[l].astype(BF16), w_ff2[l].astype(BF16)
        mod = _ada_modulation(c_all, w_ada[l], b_ada[l])
        mod_p = [m.reshape(B, 1, D) for m in jnp.split(mod[:B], N_MOD, axis=-1)]
        mod_s = [jnp.repeat(m, L, axis=0).reshape(1, DB * L, D) for m in jnp.split(mod[B:], N_MOD, axis=-1)]

        sh1, sc1, g1, sh2, sc2, g2 = mod_p
        q, k, v, hg = _in_proj(xp, sc1, sh1, norm1_gain[l], w_in_b, q_norm_gain[l], k_norm_gain[l],
                               rows_per_batch=S, head_major=True)
        ao = _moba_prompt(q, k, v, attn_out_gain[l])
        ho, s_new = _hgrn(hg, hgrn_lb_logits, hgrn_out_gain[l], None, layer=l)
        xp = _out_proj(ao.reshape(B * S, W), ho.reshape(B * S, W), w_out_b, xp, g1, rows_per_batch=S)
        xp = _ffn(xp, sc2, sh2, norm2_gain[l], g2, w1_b, w2_b, rows_per_batch=S)
        outs[0].append(k)
        outs[1].append(v)
        outs[2].append(s_new)

        sh1, sc1, g1, sh2, sc2, g2 = mod_s
        q, k, v, hg = _in_proj(xs, sc1, sh1, norm1_gain[l], w_in_b, q_norm_gain[l], k_norm_gain[l],
                               rows_per_batch=L, head_major=False)

        def heads(t):
            return t.reshape(DB, L, H, HEAD_DIM).transpose(0, 2, 1, 3)

        q, k, v = heads(q), heads(k), heads(v)
        hg = hg.reshape(4, DB, L, H, HEAD_DIM).transpose(0, 1, 3, 2, 4)
        kmean = _cache_block_means(cache_k, page_table, l, n_past_blocks, ppb)
        idx = _gate_topk(q, kmean, n_sel)
        ao = _moba_sample(q, k, v, idx, page_table, cache_k, cache_v, attn_out_gain[l], l, ppb)
        ho, s_new = _hgrn(hg, hgrn_lb_logits, hgrn_out_gain[l], state_hgrn[l], layer=l)
        xs = _out_proj(ao.reshape(DB * L, W).astype(BF16), ho.reshape(DB * L, W).astype(BF16), w_out_b, xs, g1,
                       rows_per_batch=L)
        xs = _ffn(xs, sc2, sh2, norm2_gain[l], g2, w1_b, w2_b, rows_per_batch=L)
        outs[3].append(k)
        outs[4].append(v)
        outs[5].append(s_new)

    def stack(ts):
        return ts[0][None] if len(ts) == 1 else jnp.stack(ts)

    kp, vp, sp, ks, vs, ss = (stack(t) for t in outs)
    return (xp.reshape(B, S, D), xs.reshape(DB, L, D), kp, vp, sp, ks, vs, ss)
```
